```python
import math
import jax, jax.numpy as jnp
from jax import lax
import numpy as np

D_MODEL = 1024
BATCH = 8
SEQ = 2048
DEPTH = 4
DEC_BATCH = 32
DEC_SEQ = 8
PAST_LEN = 8192
PAGE_SIZE = 128

M_HEADS = 4
M_HEAD_DIM = D_MODEL // M_HEADS
M_WIDTH = M_HEADS * M_HEAD_DIM
CONV_W = 4
CONV_CH = 2 * M_WIDTH
CHUNK = 128
SB_HEADS = 8
SB_HEAD_DIM = D_MODEL // SB_HEADS
SB_WIDTH = SB_HEADS * SB_HEAD_DIM
Q_BLOCK = 128
D_FF = ((8 * D_MODEL // 3 + 255) // 256) * 256
D_FF_EXPERT = 7 * D_MODEL // 2
N_EXPERTS = 8
TOP_K = 2
N_DENSE = (DEPTH + 1) // 2
N_MOE = DEPTH // 2
N_IN = CONV_CH + 2 * M_WIDTH + 3 * SB_WIDTH + 2 * D_MODEL + 2 * M_HEADS
ALPHA = (2 * DEPTH) ** 0.25
BETA_INIT = (8 * DEPTH) ** -0.25
LN_EPS = 1e-5

kernel_name = 'hybrid_mlstm_stickbreaking_deepnorm_step'

F32 = jnp.float32


def _layer_norm(x, g, b):
    xf = x.astype(F32)
    mu = xf.mean(-1, keepdims=True)
    var = jnp.mean(jnp.square(xf - mu), -1, keepdims=True)
    return ((xf - mu) * lax.rsqrt(var + LN_EPS) * g + b).astype(x.dtype)


def _short_conv(u, buf, w, b):
    L = u.shape[1]
    xc = jnp.concatenate([buf.astype(u.dtype), u], axis=1)
    y = sum(w[j] * xc[:, j:j + L] for j in range(CONV_W)) + b
    return jax.nn.silu(y), xc[:, -(CONV_W - 1):]


def _front(h, buf, w_in_l, b_in_l, conv_w_l, conv_b_l, b_f_l):
    B, L, _ = h.shape
    proj = h @ w_in_l + b_in_l
    sizes = (CONV_CH, M_WIDTH, M_WIDTH, SB_WIDTH, SB_WIDTH, SB_WIDTH, D_MODEL, D_MODEL, M_HEADS, M_HEADS)
    parts = []
    off = 0
    for s in sizes:
        parts.append(proj[..., off:off + s])
        off += s
    qk_pre, mv, mo, sq, sk, sv, gm, gs, mi, mf = parts
    qk, new_buf = _short_conv(qk_pre, buf, conv_w_l, conv_b_l)
    q = qk[..., :M_WIDTH].reshape(B, L, M_HEADS, M_HEAD_DIM)
    k = qk[..., M_WIDTH:].reshape(B, L, M_HEADS, M_HEAD_DIM) * (M_HEAD_DIM ** -0.5)
    v = mv.reshape(B, L, M_HEADS, M_HEAD_DIM)
    ig = mi.astype(F32)
    lf = jax.nn.log_sigmoid((mf + b_f_l).astype(F32))
    sb = [a.reshape(B, L, SB_HEADS, SB_HEAD_DIM) for a in (sq, sk, sv)]
    return (q, k, v, ig, lf), mo, sb, gm, gs, new_buf


def _mlstm_chunk(q, k, v, ig, lf, C0, n0, m0):
    q, k, v = q.astype(F32), k.astype(F32), v.astype(F32)
    C0, n0, m0 = C0.astype(F32), n0.astype(F32), m0.astype(F32)
    L = q.shape[1]
    bt = jnp.cumsum(lf, axis=1).transpose(0, 2, 1)
    it = ig.transpose(0, 2, 1)
    causal = jnp.tril(jnp.ones((L, L), dtype=bool))
    dmat = jnp.where(causal, bt[..., :, None] - bt[..., None, :] + it[..., None, :], -jnp.inf)
    inter = bt + m0[..., None]
    m = jnp.maximum(inter, dmat.max(-1))
    w_intra = jnp.exp(dmat - m[..., None])
    w_inter = jnp.exp(inter - m)
    s = jnp.einsum('blhd,bshd->bhls', q, k) * w_intra
    num = (jnp.einsum('bhls,bshe->blhe', s, v)
           + jnp.einsum('blhd,bhde->blhe', q, C0) * w_inter.transpose(0, 2, 1)[..., None])
    den = s.sum(-1) + w_inter * jnp.einsum('blhd,bhd->bhl', q, n0)
    den = jnp.maximum(jnp.abs(den), jnp.exp(-m))
    h = num / den.transpose(0, 2, 1)[..., None]
    m_last = m[..., -1]
    wk = jnp.exp(bt[..., -1:] - bt + it - m_last[..., None])
    decay = jnp.exp(bt[..., -1] + m0 - m_last)
    C = decay[..., None, None] * C0 + jnp.einsum('bhs,bshd,bshe->bhde', wk, k, v)
    n = decay[..., None] * n0 + jnp.einsum('bhs,bshd->bhd', wk, k)
    return h, (C, n, m_last)


def _mlstm_prompt(q, k, v, ig, lf):
    B, S, H, Dh = q.shape
    nc = S // CHUNK

    def to_chunks(a):
        return a.reshape((B, nc, CHUNK) + a.shape[2:]).swapaxes(0, 1)

    init = (jnp.zeros((B, H, Dh, Dh), F32), jnp.zeros((B, H, Dh), F32), jnp.zeros((B, H), F32))

    def step(carry, xs):
        qc, kc, vc, ic, fc = xs
        h, new_carry = _mlstm_chunk(qc, kc, vc, ic, fc, *carry)
        return new_carry, h

    (C, n, m), hs = lax.scan(step, init, (to_chunks(q), to_chunks(k), to_chunks(v), to_chunks(ig), to_chunks(lf)))
    return hs.swapaxes(0, 1).reshape(B, S, H, Dh), C, n, m


def _mlstm_out(h, mo, gain, dtype):
    B, L, H, Dh = h.shape
    h = jax.nn.sigmoid(mo.astype(F32)).reshape(B, L, H, Dh) * h
    mu = h.mean(-1, keepdims=True)
    var = jnp.mean(jnp.square(h - mu), -1, keepdims=True)
    h = (h - mu) * lax.rsqrt(var + LN_EPS)
    return (h.reshape(B, L, H * Dh) * gain).astype(dtype)


def _sb_block(q, k, v, q_pos, k_pos, bias):
    z = jnp.einsum('bqhd,bshd->bhqs', q, k).astype(F32) * (SB_HEAD_DIM ** -0.5) + bias.astype(F32)[None, :, None, None]
    mask = k_pos[None, :] < q_pos[:, None]
    log_1mb = jnp.where(mask, -jax.nn.softplus(z), 0.0)
    rest = lax.cumsum(log_1mb, axis=3, reverse=True) - log_1mb
    a = jnp.where(mask, jnp.exp(rest - jax.nn.softplus(-z)), 0.0)
    return jnp.einsum('bhqs,bshd->bqhd', a.astype(v.dtype), v)


def _sb_prompt(q, k, v, bias):
    B, S, H, Dh = q.shape
    nb = S // Q_BLOCK
    qb = q.reshape(B, nb, Q_BLOCK, H, Dh).swapaxes(0, 1)
    pos = jnp.arange(S)
    pb = pos.reshape(nb, Q_BLOCK)
    out = lax.map(lambda a: _sb_block(a[0], k, v, a[1], pos, bias), (qb, pb))
    return out.swapaxes(0, 1).reshape(B, S, H * Dh)


def _sb_sample(q, k, v, past_k, past_v, bias):
    B, L, H, Dh = q.shape
    P = past_k.shape[1]
    k_all = jnp.concatenate([past_k.astype(k.dtype), k], axis=1)
    v_all = jnp.concatenate([past_v.astype(v.dtype), v], axis=1)
    out = _sb_block(q, k_all, v_all, P + jnp.arange(L), jnp.arange(P + L), bias)
    return out.reshape(B, L, H * Dh)


def _merge(ym, ys, gm, gs, w_o):
    dt = ym.dtype
    mixed = jax.nn.sigmoid(gm.astype(F32)).astype(dt) * ym + jax.nn.sigmoid(gs.astype(F32)).astype(dt) * ys
    return mixed @ w_o


def _swiglu(x, w1, w3, w2):
    return (jax.nn.silu(x @ w1) * (x @ w3)) @ w2


def _moe(x, w_r, w1, w3, w2):
    probs = jax.nn.softmax((x @ w_r).astype(F32), axis=-1)
    top_p, top_i = lax.top_k(probs, TOP_K)
    top_p = top_p / top_p.sum(-1, keepdims=True)
    gates = jnp.sum(jax.nn.one_hot(top_i, N_EXPERTS, dtype=F32) * top_p[..., None], axis=-2).astype(x.dtype)
    y = jnp.zeros_like(x)
    for e in range(N_EXPERTS):
        y = y + gates[..., e:e + 1] * _swiglu(x, w1[e], w3[e], w2[e])
    return y


def _channel(l, x, ffn_w1, ffn_w3, ffn_w2, router_w, exp_w1, exp_w3, exp_w2):
    j = l // 2
    if l % 2 == 0:
        return _swiglu(x, ffn_w1[j], ffn_w3[j], ffn_w2[j])
    return _moe(x, router_w[j], exp_w1[j], exp_w3[j], exp_w2[j])


def setup_inputs(seed: int = 0) -> dict:
    key = jax.random.key(seed)
    ks = jax.random.split(key, 32)
    n_pages = PAST_LEN // PAGE_SIZE
    n_used = DEC_BATCH * n_pages
    n_pool = n_used + (n_used + 3) // 4

    def nrm(k, shape, s):
        return jax.random.normal(k, shape, F32) * s

    return {
        'x_prompt': nrm(ks[0], (BATCH, SEQ, D_MODEL), 1.0),
        'x_sample': nrm(ks[1], (DEC_BATCH, DEC_SEQ, D_MODEL), 1.0),
        'cache_k': nrm(ks[2], (DEPTH, n_pool, PAGE_SIZE, SB_HEADS, SB_HEAD_DIM), 1.0),
        'cache_v': nrm(ks[3], (DEPTH, n_pool, PAGE_SIZE, SB_HEADS, SB_HEAD_DIM), 1.0),
        'page_table': jax.random.permutation(ks[4], n_pool)[:n_used].reshape(DEC_BATCH, n_pages).astype(jnp.int32),
        'state_C': nrm(ks[5], (DEPTH, DEC_BATCH, M_HEADS, M_HEAD_DIM, M_HEAD_DIM), 0.1),
        'state_n': nrm(ks[6], (DEPTH, DEC_BATCH, M_HEADS, M_HEAD_DIM), 0.1),
        'state_m': nrm(ks[7], (DEPTH, DEC_BATCH, M_HEADS), 0.5),
        'state_conv': nrm(ks[8], (DEPTH, DEC_BATCH, CONV_W - 1, CONV_CH), 1.0),
        'w_in': nrm(ks[9], (DEPTH, D_MODEL, N_IN), D_MODEL ** -0.5),
        'b_in': nrm(ks[10], (DEPTH, N_IN), 0.02),
        'conv_w': nrm(ks[11], (DEPTH, CONV_W, CONV_CH), CONV_W ** -0.5),
        'conv_b': nrm(ks[12], (DEPTH, CONV_CH), 0.02),
        'mlstm_b_f': jnp.linspace(3.0, 6.0, M_HEADS, dtype=F32)[None, :] + nrm(ks[13], (DEPTH, M_HEADS), 0.1),
        'mlstm_gain': 1.0 + nrm(ks[14], (DEPTH, M_WIDTH), 0.02),
        'sb_bias': jnp.linspace(-8.0, -5.0, SB_HEADS, dtype=F32)[None, :] + nrm(ks[27], (DEPTH, SB_HEADS), 0.1),
        'w_out': nrm(ks[15], (DEPTH, D_MODEL, D_MODEL), D_MODEL ** -0.5 * BETA_INIT),
        'ln1_g': 1.0 + nrm(ks[16], (DEPTH, D_MODEL), 0.02),
        'ln1_b': nrm(ks[17], (DEPTH, D_MODEL), 0.02),
        'ln2_g': 1.0 + nrm(ks[18], (DEPTH, D_MODEL), 0.02),
        'ln2_b': nrm(ks[19], (DEPTH, D_MODEL), 0.02),
        'ffn_w1': nrm(ks[20], (N_DENSE, D_MODEL, D_FF), D_MODEL ** -0.5),
        'ffn_w3': nrm(ks[21], (N_DENSE, D_MODEL, D_FF), D_MODEL ** -0.5),
        'ffn_w2': nrm(ks[22], (N_DENSE, D_FF, D_MODEL), D_FF ** -0.5 * BETA_INIT),
        'router_w': nrm(ks[23], (N_MOE, D_MODEL, N_EXPERTS), D_MODEL ** -0.5),
        'exp_w1': nrm(ks[24], (N_MOE, N_EXPERTS, D_MODEL, D_FF_EXPERT), D_MODEL ** -0.5),
        'exp_w3': nrm(ks[25], (N_MOE, N_EXPERTS, D_MODEL, D_FF_EXPERT), D_MODEL ** -0.5),
        'exp_w2': nrm(ks[26], (N_MOE, N_EXPERTS, D_FF_EXPERT, D_MODEL), D_FF_EXPERT ** -0.5 * BETA_INIT),
    }


def reference(x_prompt, x_sample, cache_k, cache_v, page_table, state_C, state_n, state_m, state_conv,
              w_in, b_in, conv_w, conv_b, mlstm_b_f, mlstm_gain, sb_bias, w_out, ln1_g, ln1_b, ln2_g, ln2_b,
              ffn_w1, ffn_w3, ffn_w2, router_w, exp_w1, exp_w3, exp_w2):
    hp, hs = x_prompt, x_sample
    db = x_sample.shape[0]
    past = page_table.shape[1] * cache_k.shape[2]
    kp, vp, ksl, vsl = [], [], [], []
    cp_l, np_l, mp_l, bp_l = [], [], [], []
    cs_l, ns_l, ms_l, bs_l = [], [], [], []
    for l in range(DEPTH):
        zero_buf = jnp.zeros((hp.shape[0], CONV_W - 1, CONV_CH), hp.dtype)
        (q, k, v, ig, lf), mo, (sq, sk, sv), gm, gs, buf = _front(hp, zero_buf, w_in[l], b_in[l], conv_w[l], conv_b[l], mlstm_b_f[l])
        hm, C, n, m = _mlstm_prompt(q, k, v, ig, lf)
        ym = _mlstm_out(hm, mo, mlstm_gain[l], hp.dtype)
        ysb = _sb_prompt(sq, sk, sv, sb_bias[l])
        hp = _layer_norm(ALPHA * hp + _merge(ym, ysb, gm, gs, w_out[l]), ln1_g[l], ln1_b[l])
        hp = _layer_norm(ALPHA * hp + _channel(l, hp, ffn_w1, ffn_w3, ffn_w2, router_w, exp_w1, exp_w3, exp_w2), ln2_g[l], ln2_b[l])
        kp.append(sk)
        vp.append(sv)
        cp_l.append(C)
        np_l.append(n)
        mp_l.append(m)
        bp_l.append(buf)
        (q, k, v, ig, lf), mo, (sq, sk, sv), gm, gs, buf = _front(hs, state_conv[l], w_in[l], b_in[l], conv_w[l], conv_b[l], mlstm_b_f[l])
        hm, (C, n, m) = _mlstm_chunk(q, k, v, ig, lf, state_C[l], state_n[l], state_m[l])
        ym = _mlstm_out(hm, mo, mlstm_gain[l], hs.dtype)
        past_k = cache_k[l][page_table].reshape(db, past, SB_HEADS, SB_HEAD_DIM)
        past_v = cache_v[l][page_table].reshape(db, past, SB_HEADS, SB_HEAD_DIM)
        ysb = _sb_sample(sq, sk, sv, past_k, past_v, sb_bias[l])
        hs = _layer_norm(ALPHA * hs + _merge(ym, ysb, gm, gs, w_out[l]), ln1_g[l], ln1_b[l])
        hs = _layer_norm(ALPHA * hs + _channel(l, hs, ffn_w1, ffn_w3, ffn_w2, router_w, exp_w1, exp_w3, exp_w2), ln2_g[l], ln2_b[l])
        ksl.append(sk)
        vsl.append(sv)
        cs_l.append(C)
        ns_l.append(n)
        ms_l.append(m)
        bs_l.append(buf)
    y_prompt, y_sample = hp, hs
    new_k_prompt, new_v_prompt = jnp.stack(kp), jnp.stack(vp)
    new_k_sample, new_v_sample = jnp.stack(ksl), jnp.stack(vsl)
    new_C_prompt, new_n_prompt, new_m_prompt, new_conv_prompt = jnp.stack(cp_l), jnp.stack(np_l), jnp.stack(mp_l), jnp.stack(bp_l)
    new_C_sample, new_n_sample, new_m_sample, new_conv_sample = jnp.stack(cs_l), jnp.stack(ns_l), jnp.stack(ms_l), jnp.stack(bs_l)
    return (y_prompt, y_sample, new_k_prompt, new_v_prompt, new_k_sample, new_v_sample,
            new_C_prompt, new_n_prompt, new_m_prompt, new_conv_prompt,
            new_C_sample, new_n_sample, new_m_sample, new_conv_sample)
```

```python
import functools

import jax
import jax.numpy as jnp
from jax import lax
from jax.experimental import pallas as pl
from jax.experimental.pallas import tpu as pltpu

F32 = jnp.float32
BF16 = jnp.bfloat16
LN_EPS = 1e-5
TOP_K = 2
CHUNK = 128
Q_BLOCK = 128
SUBLANES = 8
LANES = 128
VMEM_LIMIT_BYTES = 48 * 1024 * 1024


def _params(semantics):
    return pltpu.CompilerParams(dimension_semantics=semantics, vmem_limit_bytes=VMEM_LIMIT_BYTES)


def _row_tile(m, target):
    if m <= target:
        return m
    best = SUBLANES
    for t in range(SUBLANES, target + 1, SUBLANES):
        if m % t == 0:
            best = t
    return best


def _sigmoid(x):
    return jax.nn.sigmoid(x)


def _softplus_parts(z):
    t = jnp.log1p(jnp.exp(-jnp.abs(z)))
    return jnp.maximum(z, 0.0) + t, jnp.maximum(-z, 0.0) + t


def _linear_kernel(x_ref, w_ref, b_ref, o_ref, wbf_ref):
    @pl.when(pl.program_id(1) == 0)
    def _():
        wbf_ref[...] = w_ref[...].astype(BF16)

    o_ref[...] = jnp.dot(x_ref[...], wbf_ref[...], preferred_element_type=F32) + b_ref[...]


def _linear(x, w, b, layer, n_cols, tm, tn):
    m, k = x.shape
    return pl.pallas_call(
        _linear_kernel,
        grid=(n_cols // tn, m // tm),
        in_specs=[
            pl.BlockSpec((tm, k), lambda n, i: (i, 0)),
            pl.BlockSpec((None, k, tn), lambda n, i: (layer, 0, n)),
            pl.BlockSpec((None, 1, tn), lambda n, i: (layer, 0, n)),
        ],
        out_specs=pl.BlockSpec((tm, tn), lambda n, i: (i, n)),
        out_shape=jax.ShapeDtypeStruct((m, n_cols), F32),
        scratch_shapes=[pltpu.VMEM((k, tn), BF16)],
        compiler_params=_params(("arbitrary", "arbitrary")),
        name="linear",
    )(x, w, b)


def _mlstm_kernel(qp_ref, kp_ref, v_ref, mo_ref, igr_ref, fgr_ref, igc_ref, fgc_ref,
                  bq_ref, bk_ref, cwq_ref, cwk_ref, cbq_ref, cbk_ref, c0_ref, n0_ref, m0_ref, gain_ref,
                  *rest, L, Dh, CW, aliased):
    if aliased:
        rest = rest[1:]
    ym_ref, c_ref, n_ref, m_ref, cs, ns, ms, qbuf, kbuf = rest
    c = pl.program_id(2)
    nc = pl.num_programs(2)

    @pl.when(c == 0)
    def _():
        cs[...] = c0_ref[...]
        ns[...] = n0_ref[...]
        ms[...] = m0_ref[...]
        qbuf[0:SUBLANES, :] = bq_ref[...]
        kbuf[0:SUBLANES, :] = bk_ref[...]

    @pl.when(c > 0)
    def _():
        qbuf[0:SUBLANES, :] = qbuf[L:L + SUBLANES, :]
        kbuf[0:SUBLANES, :] = kbuf[L:L + SUBLANES, :]

    qbuf[SUBLANES:SUBLANES + L, :] = qp_ref[...]
    kbuf[SUBLANES:SUBLANES + L, :] = kp_ref[...]

    def conv(buf, w_ref, b_ref):
        base = SUBLANES - (CW - 1)
        y = b_ref[...] + w_ref[0:1, :] * buf[base:base + L, :]
        for j in range(1, CW):
            y = y + w_ref[j:j + 1, :] * buf[base + j:base + j + L, :]
        return y * _sigmoid(y)

    q = conv(qbuf, cwq_ref, cbq_ref)
    k = conv(kbuf, cwk_ref, cbk_ref) * (Dh ** -0.5)
    qb = q.astype(BF16)
    kb = k.astype(BF16)
    vb = v_ref[...].astype(BF16)

    def log_sigmoid(x):
        return jnp.minimum(x, 0.0) - jnp.log1p(jnp.exp(-jnp.abs(x)))

    ig_r = igr_ref[...]
    ig_c = igc_ref[...]
    lf_r = log_sigmoid(fgr_ref[...])
    lf_c = log_sigmoid(fgc_ref[...])
    row = lax.broadcasted_iota(jnp.int32, (L, L), 0)
    col = lax.broadcasted_iota(jnp.int32, (L, L), 1)
    causal = col <= row
    bt_c = jnp.sum(jnp.where(causal, lf_r, 0.0), axis=1, keepdims=True)
    bt_r = jnp.sum(jnp.where(row <= col, lf_c, 0.0), axis=0, keepdims=True)
    m0 = ms[...]
    dmat = jnp.where(causal, bt_c - bt_r + ig_r, -jnp.inf)
    inter = bt_c + m0
    m_c = jnp.maximum(inter, jnp.max(dmat, axis=1, keepdims=True))
    w_intra = jnp.exp(dmat - m_c)
    w_inter = jnp.exp(inter - m_c)
    nt = (((1,), (1,)), ((), ()))
    s = lax.dot_general(qb, kb, nt, preferred_element_type=F32) * w_intra
    c0b = cs[...].astype(BF16)
    num = (jnp.dot(s.astype(BF16), vb, preferred_element_type=F32)
           + jnp.dot(qb, c0b, preferred_element_type=F32) * w_inter)
    den = jnp.sum(s, axis=1, keepdims=True) + w_inter * jnp.sum(q * ns[...], axis=1, keepdims=True)
    den = jnp.maximum(jnp.abs(den), jnp.exp(-m_c))
    h = num / den

    last = lax.broadcasted_iota(jnp.int32, (L, 1), 0) == L - 1
    m_last = jnp.sum(jnp.where(last, m_c, 0.0), axis=0, keepdims=True)
    bt_last = jnp.sum(jnp.where(last, bt_c, 0.0), axis=0, keepdims=True)
    wk_c = jnp.exp(bt_last - bt_c + ig_c - m_last)
    decay = jnp.exp(bt_last + m0 - m_last)
    kw = k * wk_c
    tn = (((0,), (0,)), ((), ()))
    c_new = decay * cs[...] + lax.dot_general(kw.astype(BF16), vb, tn, preferred_element_type=F32)
    n_new = decay * ns[...] + jnp.sum(kw, axis=0, keepdims=True)
    cs[...] = c_new
    ns[...] = n_new
    ms[...] = m_last

    hh = _sigmoid(mo_ref[...]) * h
    mu = jnp.mean(hh, axis=1, keepdims=True)
    d = hh - mu
    var = jnp.mean(d * d, axis=1, keepdims=True)
    ym_ref[...] = d * lax.rsqrt(var + LN_EPS) * gain_ref[...]

    @pl.when(c == nc - 1)
    def _():
        c_ref[...] = c_new
        n_ref[...] = n_new
        m_ref[...] = m_last


def _mlstm(proj, gates, row_off, nb, seq, L, H, Dh, conv_pad, cw_pad, cb, c0, n0, m0, gain, ym_prev, CW):
    mtot = proj.shape[0]
    nc = seq // L
    rb = row_off // L
    g = gates[row_off:row_off + nb * seq]
    ig = g[:, 0:H].reshape(nb, nc, L, H).transpose(0, 3, 1, 2)
    fg = g[:, H:2 * H].reshape(nb, nc, L, H).transpose(0, 3, 1, 2)
    igr, fgr = ig[:, :, :, None, :], fg[:, :, :, None, :]
    igc, fgc = ig[..., None], fg[..., None]
    aliased = ym_prev is not None

    def colspec(off):
        return pl.BlockSpec((L, Dh), lambda b, h, c: (rb + b * nc + c, off + h))

    grow = pl.BlockSpec((None, None, None, 1, L), lambda b, h, c: (b, h, c, 0, 0))
    gcol = pl.BlockSpec((None, None, None, L, 1), lambda b, h, c: (b, h, c, 0, 0))
    in_specs = [
        colspec(0), colspec(H), colspec(2 * H), colspec(3 * H),
        grow, grow, gcol, gcol,
        pl.BlockSpec((None, SUBLANES, Dh), lambda b, h, c: (b, 0, h)),
        pl.BlockSpec((None, SUBLANES, Dh), lambda b, h, c: (b, 0, H + h)),
        pl.BlockSpec((SUBLANES, Dh), lambda b, h, c: (0, h)),
        pl.BlockSpec((SUBLANES, Dh), lambda b, h, c: (0, H + h)),
        pl.BlockSpec((1, Dh), lambda b, h, c: (0, h)),
        pl.BlockSpec((1, Dh), lambda b, h, c: (0, H + h)),
        pl.BlockSpec((None, None, Dh, Dh), lambda b, h, c: (b, h, 0, 0)),
        pl.BlockSpec((None, None, 1, Dh), lambda b, h, c: (b, h, 0, 0)),
        pl.BlockSpec((None, None, 1, 1), lambda b, h, c: (b, h, 0, 0)),
        pl.BlockSpec((1, Dh), lambda b, h, c: (0, h)),
    ]
    args = [proj, proj, proj, proj, igr, fgr, igc, fgc, conv_pad, conv_pad, cw_pad, cw_pad, cb, cb,
            c0, n0, m0, gain]
    aliases = {}
    if aliased:
        in_specs.append(pl.BlockSpec(memory_space=pl.ANY))
        args.append(ym_prev)
        aliases = {len(args) - 1: 0}
    out_specs = [
        pl.BlockSpec((L, Dh), lambda b, h, c: (rb + b * nc + c, h)),
        pl.BlockSpec((None, None, Dh, Dh), lambda b, h, c: (b, h, 0, 0)),
        pl.BlockSpec((None, None, 1, Dh), lambda b, h, c: (b, h, 0, 0)),
        pl.BlockSpec((None, None, 1, 1), lambda b, h, c: (b, h, 0, 0)),
    ]
    out_shape = [
        jax.ShapeDtypeStruct((mtot, H * Dh), F32),
        jax.ShapeDtypeStruct((nb, H, Dh, Dh), F32),
        jax.ShapeDtypeStruct((nb, H, 1, Dh), F32),
        jax.ShapeDtypeStruct((nb, H, 1, 1), F32),
    ]
    return pl.pallas_call(
        functools.partial(_mlstm_kernel, L=L, Dh=Dh, CW=CW, aliased=aliased),
        grid=(nb, H, nc),
        in_specs=in_specs,
        out_specs=out_specs,
        out_shape=out_shape,
        scratch_shapes=[
            pltpu.VMEM((Dh, Dh), F32), pltpu.VMEM((1, Dh), F32), pltpu.VMEM((1, 1), F32),
            pltpu.VMEM((L + SUBLANES, Dh), F32), pltpu.VMEM((L + SUBLANES, Dh), F32),
        ],
        input_output_aliases=aliases,
        compiler_params=_params(("arbitrary", "arbitrary", "arbitrary")),
        name="mlstm",
    )(*args)


def _sb_block(qb, kj, vj, bias, scale, tri, carry, acc, mask):
    nt = (((1,), (1,)), ((), ()))
    z = lax.dot_general(qb, kj, nt, preferred_element_type=F32) * scale + bias
    sp_pos, sp_neg = _softplus_parts(z)
    l1 = -sp_pos
    if mask is not None:
        l1 = jnp.where(mask, l1, 0.0)
    hi = l1.astype(BF16)
    lo = (l1 - hi.astype(F32)).astype(BF16)
    rest = (jnp.dot(hi, tri, preferred_element_type=F32) + jnp.dot(lo, tri, preferred_element_type=F32)) + carry
    a = jnp.exp(rest - sp_neg)
    if mask is not None:
        a = jnp.where(mask, a, 0.0)
    acc = acc + jnp.dot(a.astype(BF16), vj, preferred_element_type=F32)
    carry = carry + jnp.sum(l1, axis=1, keepdims=True)
    return carry, acc


def _sb_prompt_kernel(bias_ref, q_ref, k_ref, v_ref, o_ref, *, QB, Dh, scale):
    h = pl.program_id(1)
    i = pl.program_id(2)
    bias = bias_ref[h]
    qb = q_ref[...].astype(BF16)
    row = lax.broadcasted_iota(jnp.int32, (QB, QB), 0)
    col = lax.broadcasted_iota(jnp.int32, (QB, QB), 1)
    tri = jnp.where(row > col, 1.0, 0.0).astype(BF16)
    diag_mask = col < row

    def kv(j):
        start = pl.multiple_of(j * QB, QB)
        return k_ref[pl.ds(start, QB), :].astype(BF16), v_ref[pl.ds(start, QB), :].astype(BF16)

    kj, vj = kv(i)
    carry, acc = _sb_block(qb, kj, vj, bias, scale, tri, jnp.zeros((QB, 1), F32), jnp.zeros((QB, Dh), F32), diag_mask)

    def body(t, ca):
        kj, vj = kv(i - 1 - t)
        return _sb_block(qb, kj, vj, bias, scale, tri, ca[0], ca[1], None)

    carry, acc = lax.fori_loop(0, i, body, (carry, acc))
    o_ref[...] = acc


def _sb_prompt(proj, bias, nb, seq, H, Dh, q_off, k_off, v_off):
    mtot = proj.shape[0]
    QB = Q_BLOCK
    nq = seq // QB
    return pl.pallas_call(
        functools.partial(_sb_prompt_kernel, QB=QB, Dh=Dh, scale=Dh ** -0.5),
        grid=(nb, H, nq),
        in_specs=[
            pl.BlockSpec(memory_space=pltpu.SMEM),
            pl.BlockSpec((QB, Dh), lambda b, h, i: (b * nq + i, q_off + h)),
            pl.BlockSpec((seq, Dh), lambda b, h, i: (b, k_off + h)),
            pl.BlockSpec((seq, Dh), lambda b, h, i: (b, v_off + h)),
        ],
        out_specs=pl.BlockSpec((QB, Dh), lambda b, h, i: (b * nq + i, h)),
        out_shape=jax.ShapeDtypeStruct((mtot, H * Dh), F32),
        compiler_params=_params(("arbitrary", "arbitrary", "arbitrary")),
        name="sb_prompt",
    )(bias, proj, proj, proj)


def _sb_sample_kernel(pt_ref, q_ref, kn_ref, vn_ref, kc_ref, vc_ref, bias_ref, ysb_in_ref, o_ref,
                      wt_ref, knew_ref, vnew_ref, carry_ref, acc_ref, *, T, H, Dh, PAGE, scale):
    del pt_ref, ysb_in_ref
    p = pl.program_id(1)
    D = H * Dh
    HT = H * T
    row = lax.broadcasted_iota(jnp.int32, (PAGE, PAGE), 0)
    col = lax.broadcasted_iota(jnp.int32, (PAGE, PAGE), 1)
    tri = jnp.where(col > row, 1.0, 0.0).astype(BF16)
    nt = (((1,), (1,)), ((), ()))

    def block(kp, vp, mask):
        z = lax.dot_general(kp, wt_ref[...], nt, preferred_element_type=F32) * scale + bias_ref[...]
        sp_pos, sp_neg = _softplus_parts(z)
        l1 = -sp_pos
        if mask is not None:
            l1 = jnp.where(mask, l1, 0.0)
        hi = l1.astype(BF16)
        lo = (l1 - hi.astype(F32)).astype(BF16)
        rest = (jnp.dot(tri, hi, preferred_element_type=F32) + jnp.dot(tri, lo, preferred_element_type=F32)
                + carry_ref[...])
        a = jnp.exp(rest - sp_neg)
        if mask is not None:
            a = jnp.where(mask, a, 0.0)
        acc_ref[...] += jnp.dot(a.T.astype(BF16), vp, preferred_element_type=F32)
        carry_ref[...] += jnp.sum(l1, axis=0, keepdims=True)

    @pl.when(p == 0)
    def _():
        q = q_ref[...]
        qrep = jnp.concatenate([q] * H, axis=0)
        r = lax.broadcasted_iota(jnp.int32, (HT, D), 0) // T
        cc = lax.broadcasted_iota(jnp.int32, (HT, D), 1) // Dh
        wt_ref[0:HT, :] = jnp.where(r == cc, qrep, 0.0).astype(BF16)
        wt_ref[HT:LANES, :] = jnp.zeros((LANES - HT, D), BF16)
        knew_ref[...] = jnp.zeros((PAGE, D), BF16)
        vnew_ref[...] = jnp.zeros((PAGE, D), BF16)
        knew_ref[0:T, :] = kn_ref[...].astype(BF16)
        vnew_ref[0:T, :] = vn_ref[...].astype(BF16)
        carry_ref[...] = jnp.zeros((1, LANES), F32)
        acc_ref[...] = jnp.zeros((LANES, D), F32)
        lane_t = lax.rem(lax.broadcasted_iota(jnp.int32, (PAGE, LANES), 1), T)
        key_s = lax.broadcasted_iota(jnp.int32, (PAGE, LANES), 0)
        block(knew_ref[...], vnew_ref[...], key_s < lane_t)

    @pl.when(p > 0)
    def _():
        block(kc_ref[...].astype(BF16), vc_ref[...].astype(BF16), None)

    @pl.when(p == pl.num_programs(1) - 1)
    def _():
        for h in range(H):
            o_ref[:, h * Dh:(h + 1) * Dh] = acc_ref[h * T:(h + 1) * T, h * Dh:(h + 1) * Dh]


def _sb_sample(proj, cache_k, cache_v, page_table, bias_row, ysb, layer, row_off, nb, T, H, Dh):
    D = H * Dh
    n_pages = page_table.shape[1]
    page = cache_k.shape[2]
    rb = row_off // T
    pt = page_table.reshape(-1)

    def page_idx(b, p, pt_ref):
        return pt_ref[b * n_pages + jnp.minimum(n_pages - p, n_pages - 1)]

    grid_spec = pltpu.PrefetchScalarGridSpec(
        num_scalar_prefetch=1,
        grid=(nb, n_pages + 1),
        in_specs=[
            pl.BlockSpec((T, D), lambda b, p, pt_ref: (rb + b, 4)),
            pl.BlockSpec((T, D), lambda b, p, pt_ref: (rb + b, 5)),
            pl.BlockSpec((T, D), lambda b, p, pt_ref: (rb + b, 6)),
            pl.BlockSpec((None, None, page, D), lambda b, p, pt_ref: (layer, page_idx(b, p, pt_ref), 0, 0)),
            pl.BlockSpec((None, None, page, D), lambda b, p, pt_ref: (layer, page_idx(b, p, pt_ref), 0, 0)),
            pl.BlockSpec((1, LANES), lambda b, p, pt_ref: (0, 0)),
            pl.BlockSpec(memory_space=pl.ANY),
        ],
        out_specs=pl.BlockSpec((T, D), lambda b, p, pt_ref: (rb + b, 0)),
        scratch_shapes=[
            pltpu.VMEM((LANES, D), BF16), pltpu.VMEM((page, D), BF16), pltpu.VMEM((page, D), BF16),
            pltpu.VMEM((1, LANES), F32), pltpu.VMEM((LANES, D), F32),
        ],
    )
    return pl.pallas_call(
        functools.partial(_sb_sample_kernel, T=T, H=H, Dh=Dh, PAGE=page, scale=Dh ** -0.5),
        grid_spec=grid_spec,
        out_shape=jax.ShapeDtypeStruct(ysb.shape, F32),
        input_output_aliases={7: 0},
        compiler_params=_params(("arbitrary", "arbitrary")),
        name="sb_sample",
    )(pt, proj, proj, proj, cache_k, cache_v, bias_row, ysb)


def _layer_norm(x, g, b):
    mu = jnp.mean(x, axis=-1, keepdims=True)
    d = x - mu
    var = jnp.mean(d * d, axis=-1, keepdims=True)
    return d * lax.rsqrt(var + LN_EPS) * g + b


def _merge_kernel(ym_ref, ys_ref, gm_ref, gs_ref, w_ref, h_ref, g_ref, b_ref, o_ref, obf_ref, wbf_ref, *, alpha):
    @pl.when(pl.program_id(0) == 0)
    def _():
        wbf_ref[...] = w_ref[...].astype(BF16)

    mixed = _sigmoid(gm_ref[...]) * ym_ref[...] + _sigmoid(gs_ref[...]) * ys_ref[...]
    y = jnp.dot(mixed.astype(BF16), wbf_ref[...], preferred_element_type=F32)
    out = _layer_norm(alpha * h_ref[...] + y, g_ref[...], b_ref[...])
    o_ref[...] = out
    obf_ref[...] = out.astype(BF16)


def _merge(ym, ysb, proj, w_out, h, g, b, layer, alpha, tm, gm_blk, gs_blk):
    m, d = h.shape
    row = lambda i: (i, 0)
    vec = pl.BlockSpec((None, 1, d), lambda i: (layer, 0, 0))
    return pl.pallas_call(
        functools.partial(_merge_kernel, alpha=alpha),
        grid=(m // tm,),
        in_specs=[
            pl.BlockSpec((tm, d), row), pl.BlockSpec((tm, d), row),
            pl.BlockSpec((tm, d), lambda i: (i, gm_blk)), pl.BlockSpec((tm, d), lambda i: (i, gs_blk)),
            pl.BlockSpec((None, d, d), lambda i: (layer, 0, 0)),
            pl.BlockSpec((tm, d), row), vec, vec,
        ],
        out_specs=[pl.BlockSpec((tm, d), row), pl.BlockSpec((tm, d), row)],
        out_shape=[jax.ShapeDtypeStruct((m, d), F32), jax.ShapeDtypeStruct((m, d), BF16)],
        scratch_shapes=[pltpu.VMEM((d, d), BF16)],
        compiler_params=_params(("arbitrary",)),
        name="merge",
    )(ym, ysb, proj, proj, w_out, h, g, b)


def _swiglu_kernel(te_ref, nu_ref, x_ref, w1_ref, w3_ref, w2_ref, o_ref):
    del te_ref
    i = pl.program_id(0)
    f = pl.program_id(1)
    used = i < nu_ref[0]

    @pl.when(jnp.logical_and(f == 0, jnp.logical_not(used)))
    def _():
        o_ref[...] = jnp.zeros_like(o_ref)

    @pl.when(used)
    def _():
        x = x_ref[...]
        a = jnp.dot(x, w1_ref[...], preferred_element_type=F32)
        g = jnp.dot(x, w3_ref[...], preferred_element_type=F32)
        hmid = (a * _sigmoid(a) * g).astype(BF16)
        y = jnp.dot(hmid, w2_ref[...], preferred_element_type=F32)

        @pl.when(f == 0)
        def _():
            o_ref[...] = y

        @pl.when(f > 0)
        def _():
            o_ref[...] += y


def _swiglu(x, w1, w3, w2, tile_expert, n_used, tm, tf):
    m, d = x.shape
    ff = w1.shape[2]
    nf = ff // tf

    def fidx(i, f, nu_ref):
        return jnp.where(i < nu_ref[0], f, nf - 1)

    grid_spec = pltpu.PrefetchScalarGridSpec(
        num_scalar_prefetch=2,
        grid=(m // tm, nf),
        in_specs=[
            pl.BlockSpec((tm, d), lambda i, f, te, nu: (jnp.minimum(i, nu[0] - 1), 0)),
            pl.BlockSpec((None, d, tf), lambda i, f, te, nu: (te[i], 0, fidx(i, f, nu))),
            pl.BlockSpec((None, d, tf), lambda i, f, te, nu: (te[i], 0, fidx(i, f, nu))),
            pl.BlockSpec((None, tf, d), lambda i, f, te, nu: (te[i], fidx(i, f, nu), 0)),
        ],
        out_specs=pl.BlockSpec((tm, d), lambda i, f, te, nu: (i, 0)),
    )
    return pl.pallas_call(
        _swiglu_kernel,
        grid_spec=grid_spec,
        out_shape=jax.ShapeDtypeStruct((m, d), F32),
        compiler_params=_params(("arbitrary", "arbitrary")),
        name="swiglu",
    )(tile_expert, n_used, x, w1, w3, w2)


def _router_kernel(x_ref, w_ref, g_ref, i_ref, *, E):
    logits = jnp.dot(x_ref[...], w_ref[...].astype(BF16), preferred_element_type=F32)
    lane = lax.broadcasted_iota(jnp.int32, logits.shape, 1)
    lane_f = lane.astype(F32)
    valid = lane < E
    logits = jnp.where(valid, logits, -jnp.inf)
    ex = jnp.exp(logits - jnp.max(logits, axis=1, keepdims=True))
    probs = ex / jnp.sum(ex, axis=1, keepdims=True)
    p1 = jnp.max(probs, axis=1, keepdims=True)
    i1 = jnp.min(jnp.where(probs == p1, lane_f, float(LANES)), axis=1, keepdims=True)
    rest = jnp.where(jnp.logical_or(lane_f == i1, jnp.logical_not(valid)), -1.0, probs)
    p2 = jnp.max(rest, axis=1, keepdims=True)
    i2 = jnp.min(jnp.where(rest == p2, lane_f, float(LANES)), axis=1, keepdims=True)
    tot = p1 + p2
    g_ref[...] = jnp.where(lane == 0, p1 / tot, jnp.where(lane == 1, p2 / tot, 0.0))
    i_ref[...] = jnp.where(lane == 0, i1, jnp.where(lane == 1, i2, 0.0)).astype(jnp.int32)


def _router(x, w_pad, E, tm):
    m, d = x.shape
    return pl.pallas_call(
        functools.partial(_router_kernel, E=E),
        grid=(m // tm,),
        in_specs=[pl.BlockSpec((tm, d), lambda i: (i, 0)), pl.BlockSpec((d, LANES), lambda i: (0, 0))],
        out_specs=[pl.BlockSpec((tm, LANES), lambda i: (i, 0)), pl.BlockSpec((tm, LANES), lambda i: (i, 0))],
        out_shape=[jax.ShapeDtypeStruct((m, LANES), F32), jax.ShapeDtypeStruct((m, LANES), jnp.int32)],
        compiler_params=_params(("arbitrary",)),
        name="router",
    )(x, w_pad)


def _resid_ln_kernel(*refs, alpha, moe):
    if moe:
        x_ref, ya_ref, yb_ref, gt_ref, g_ref, b_ref, o_ref, obf_ref = refs
        gt = gt_ref[...]
        y = gt[:, 0:1] * ya_ref[...] + gt[:, 1:2] * yb_ref[...]
    else:
        x_ref, ya_ref, g_ref, b_ref, o_ref, obf_ref = refs
        y = ya_ref[...]
    out = _layer_norm(alpha * x_ref[...] + y, g_ref[...], b_ref[...])
    o_ref[...] = out
    obf_ref[...] = out.astype(BF16)


def _resid_ln(x, ys, gates, g, b, layer, alpha, tm):
    m, d = x.shape
    row = pl.BlockSpec((tm, d), lambda i: (i, 0))
    vec = pl.BlockSpec((None, 1, d), lambda i: (layer, 0, 0))
    moe = gates is not None
    in_specs = [row] + [row] * len(ys) + ([pl.BlockSpec((tm, LANES), lambda i: (i, 0))] if moe else []) + [vec, vec]
    args = [x, *ys] + ([gates] if moe else []) + [g, b]
    return pl.pallas_call(
        functools.partial(_resid_ln_kernel, alpha=alpha, moe=moe),
        grid=(m // tm,),
        in_specs=in_specs,
        out_specs=[row, row],
        out_shape=[jax.ShapeDtypeStruct((m, d), F32), jax.ShapeDtypeStruct((m, d), BF16)],
        compiler_params=_params(("arbitrary",)),
        name="resid_ln",
    )(*args)


def _moe_plan(idx, E, tm):
    m = idx.shape[0]
    flat_e = idx.reshape(-1)
    onehot = (flat_e[:, None] == jnp.arange(E, dtype=jnp.int32)[None, :]).astype(jnp.int32)
    ranks = jnp.cumsum(onehot, axis=0) - onehot
    rank = jnp.sum(ranks * onehot, axis=1)
    counts = jnp.sum(onehot, axis=0)
    padded = ((counts + tm - 1) // tm) * tm
    ends = jnp.cumsum(padded)
    starts = ends - padded
    dest = starts[flat_e] + rank
    n_rows = TOP_K * m + E * tm
    n_rows = ((n_rows + tm - 1) // tm) * tm
    row_token = jnp.zeros((n_rows,), jnp.int32).at[dest].set(jnp.arange(TOP_K * m, dtype=jnp.int32) // TOP_K)
    tile_start = jnp.arange(n_rows // tm, dtype=jnp.int32) * tm
    tile_expert = jnp.minimum(jnp.searchsorted(ends, tile_start, side="right"), E - 1).astype(jnp.int32)
    n_used = (ends[-1] // tm).astype(jnp.int32).reshape(1)
    return dest.reshape(m, TOP_K), row_token, tile_expert, n_used


def _ff_tile(ff, target):
    best = LANES
    for t in range(LANES, min(ff, target) + 1, LANES):
        if ff % t == 0:
            best = t
    return best


def kernel(x_prompt, x_sample, cache_k, cache_v, page_table, state_C, state_n, state_m, state_conv, w_in, b_in, conv_w, conv_b, mlstm_b_f, mlstm_gain, sb_bias, w_out, ln1_g, ln1_b, ln2_g, ln2_b, ffn_w1, ffn_w3, ffn_w2, router_w, exp_w1, exp_w3, exp_w2):
    B, S, D = x_prompt.shape
    DB, T, _ = x_sample.shape
    depth = w_in.shape[0]
    H = mlstm_b_f.shape[1]
    Dh = D // H
    SH = sb_bias.shape[1]
    SDh = D // SH
    CW = conv_w.shape[1]
    E = router_w.shape[2]
    n_pool, page = cache_k.shape[1], cache_k.shape[2]
    alpha = (2 * depth) ** 0.25
    n_main = 9 * D
    assert w_in.shape[2] == n_main + 2 * H and 2 * H <= LANES and SH * T <= LANES
    mp, ms = B * S, DB * T
    mtot = mp + ms
    tm_lin = _row_tile(mtot, 1280)
    tm_row = _row_tile(mtot, 640)
    tm_moe = 512
    tn_lin = _ff_tile(n_main, 1024)

    h = jnp.concatenate([x_prompt.reshape(mp, D), x_sample.reshape(ms, D)], axis=0)
    hbf = h.astype(BF16)
    cache_k = cache_k.reshape(depth, n_pool, page, D)
    cache_v = cache_v.reshape(depth, n_pool, page, D)
    b_in3 = b_in.reshape(depth, 1, -1)
    ln1_g3, ln1_b3 = ln1_g.reshape(depth, 1, D), ln1_b.reshape(depth, 1, D)
    ln2_g3, ln2_b3 = ln2_g.reshape(depth, 1, D), ln2_b.reshape(depth, 1, D)
    ffn_w1b, ffn_w3b, ffn_w2b = ffn_w1.astype(BF16), ffn_w3.astype(BF16), ffn_w2.astype(BF16)
    exp_w1b, exp_w3b, exp_w2b = exp_w1.astype(BF16), exp_w3.astype(BF16), exp_w2.astype(BF16)
    zero_c = jnp.zeros((B, H, Dh, Dh), F32)
    zero_n = jnp.zeros((B, H, 1, Dh), F32)
    zero_m = jnp.zeros((B, H, 1, 1), F32)
    zero_conv = jnp.zeros((B, SUBLANES, 2 * D), F32)
    dense_te = jnp.zeros((mtot // tm_row,), jnp.int32)
    dense_nu = jnp.full((1,), mtot // tm_row, jnp.int32)

    outs = {k: [] for k in ("kp", "vp", "ks", "vs", "cp", "np", "mp", "bp", "cs", "ns", "ms", "bs")}
    for l in range(depth):
        proj = _linear(hbf, w_in, b_in3, l, n_main, tm_lin, tn_lin)
        wg = jnp.pad(w_in[l][:, n_main:], ((0, 0), (0, LANES - 2 * H)))[None]
        bg = jnp.pad(b_in[l][n_main:] + jnp.concatenate([jnp.zeros((H,), F32), mlstm_b_f[l]]), (0, LANES - 2 * H))
        gates = _linear(hbf, wg, bg.reshape(1, 1, LANES), 0, LANES, tm_lin, LANES)

        cw_pad = jnp.pad(conv_w[l], ((0, SUBLANES - CW), (0, 0)))
        cb = conv_b[l].reshape(1, -1)
        gain = mlstm_gain[l].reshape(1, -1)
        conv_s = jnp.pad(state_conv[l], ((0, 0), (SUBLANES - (CW - 1), 0), (0, 0)))
        ym, c_p, n_p, m_p = _mlstm(proj, gates, 0, B, S, CHUNK, H, Dh, zero_conv, cw_pad, cb,
                                   zero_c, zero_n, zero_m, gain, None, CW)
        ym, c_s, n_s, m_s = _mlstm(proj, gates, mp, DB, T, T, H, Dh, conv_s, cw_pad, cb,
                                   state_C[l], state_n[l].reshape(DB, H, 1, Dh), state_m[l].reshape(DB, H, 1, 1),
                                   gain, ym, CW)

        ysb = _sb_prompt(proj, sb_bias[l], B, S, SH, SDh, 4 * SH, 5 * SH, 6 * SH)
        bias_row = jnp.pad(jnp.repeat(sb_bias[l], T), (0, LANES - SH * T)).reshape(1, LANES)
        ysb = _sb_sample(proj, cache_k, cache_v, page_table, bias_row, ysb, l, mp, DB, T, SH, SDh)

        h, hbf = _merge(ym, ysb, proj, w_out, h, ln1_g3, ln1_b3, l, alpha, tm_row, 7, 8)

        j = l // 2
        if l % 2 == 0:
            tf = _ff_tile(ffn_w1.shape[2], 1408)
            y = _swiglu(hbf, ffn_w1b[j:j + 1], ffn_w3b[j:j + 1], ffn_w2b[j:j + 1], dense_te, dense_nu, tm_row, tf)
            h, hbf = _resid_ln(h, [y], None, ln2_g3, ln2_b3, l, alpha, tm_row)
        else:
            wr = jnp.pad(router_w[j], ((0, 0), (0, LANES - E)))
            gts, idx = _router(hbf, wr, E, tm_row)
            pos, row_token, tile_expert, n_used = _moe_plan(idx[:, :TOP_K], E, tm_moe)
            xs = jnp.take(hbf, row_token, axis=0)
            tf = _ff_tile(exp_w1.shape[3], 512)
            ysorted = _swiglu(xs, exp_w1b[j], exp_w3b[j], exp_w2b[j], tile_expert, n_used, tm_moe, tf)
            ya = jnp.take(ysorted, pos[:, 0], axis=0)
            yb = jnp.take(ysorted, pos[:, 1], axis=0)
            h, hbf = _resid_ln(h, [ya, yb], gts, ln2_g3, ln2_b3, l, alpha, tm_row)

        kcol, vcol = slice(5 * D, 6 * D), slice(6 * D, 7 * D)
        outs["kp"].append(proj[:mp, kcol].reshape(B, S, SH, SDh))
        outs["vp"].append(proj[:mp, vcol].reshape(B, S, SH, SDh))
        outs["ks"].append(proj[mp:, kcol].reshape(DB, T, SH, SDh))
        outs["vs"].append(proj[mp:, vcol].reshape(DB, T, SH, SDh))
        outs["cp"].append(c_p)
        outs["np"].append(n_p.reshape(B, H, Dh))
        outs["mp"].append(m_p.reshape(B, H))
        qk_p = proj[:mp, :2 * D].reshape(B, S, 2 * D)
        outs["bp"].append(qk_p[:, S - (CW - 1):])
        outs["cs"].append(c_s)
        outs["ns"].append(n_s.reshape(DB, H, Dh))
        outs["ms"].append(m_s.reshape(DB, H))
        qk_s = proj[mp:, :2 * D].reshape(DB, T, 2 * D)
        outs["bs"].append(jnp.concatenate([state_conv[l], qk_s], axis=1)[:, -(CW - 1):])

    st = {k: jnp.stack(v) for k, v in outs.items()}
    return (h[:mp].reshape(B, S, D), h[mp:].reshape(DB, T, D),
            st["kp"], st["vp"], st["ks"], st["vs"],
            st["cp"], st["np"], st["mp"], st["bp"],
            st["cs"], st["ns"], st["ms"], st["bs"])
```

```python
import functools

import jax
import jax.numpy as jnp
from jax import lax
from jax.experimental import pallas as pl
from jax.experimental.pallas import tpu as pltpu

F32 = jnp.float32
BF16 = jnp.bfloat16
LN_EPS = 1e-5
TOP_K = 2
CHUNK = 128
Q_BLOCK = 128
SB_Q_TILE = 512
SUBLANES = 8
LANES = 128
VMEM_LIMIT_BYTES = 48 * 1024 * 1024


def _params(semantics):
    return pltpu.CompilerParams(dimension_semantics=semantics, vmem_limit_bytes=VMEM_LIMIT_BYTES)


def _row_tile(m, target):
    if m <= target:
        return m
    best = SUBLANES
    for t in range(SUBLANES, target + 1, SUBLANES):
        if m % t == 0:
            best = t
    return best


def _sigmoid(x):
    return jax.nn.sigmoid(x)


def _softplus_parts(z):
    t = jnp.log1p(jnp.exp(-jnp.abs(z)))
    return jnp.maximum(z, 0.0) + t, jnp.maximum(-z, 0.0) + t


def _linear_kernel(x_ref, w_ref, b_ref, o_ref, wbf_ref):
    @pl.when(pl.program_id(1) == 0)
    def _():
        wbf_ref[...] = w_ref[...].astype(BF16)

    o_ref[...] = jnp.dot(x_ref[...], wbf_ref[...], preferred_element_type=F32) + b_ref[...]


def _linear(x, w, b, layer, n_cols, tm, tn):
    m, k = x.shape
    return pl.pallas_call(
        _linear_kernel,
        grid=(n_cols // tn, m // tm),
        in_specs=[
            pl.BlockSpec((tm, k), lambda n, i: (i, 0)),
            pl.BlockSpec((None, k, tn), lambda n, i: (layer, 0, n)),
            pl.BlockSpec((None, 1, tn), lambda n, i: (layer, 0, n)),
        ],
        out_specs=pl.BlockSpec((tm, tn), lambda n, i: (i, n)),
        out_shape=jax.ShapeDtypeStruct((m, n_cols), F32),
        scratch_shapes=[pltpu.VMEM((k, tn), BF16)],
        compiler_params=_params(("arbitrary", "arbitrary")),
        name="linear",
    )(x, w, b)


def _mlstm_kernel(qp_ref, kp_ref, v_ref, mo_ref, igr_ref, fgr_ref, igc_ref, fgc_ref,
                  buf_ref, cw_ref, cb_ref, c0_ref, n0_ref, m0_ref, gain_ref,
                  *rest, L, H, Dh, CW, aliased):
    if aliased:
        rest = rest[1:]
    ym_ref, c_ref, n_ref, m_ref, cs, ns, ms, qbuf, kbuf = rest
    c = pl.program_id(1)
    nc = pl.num_programs(1)
    MW = H * Dh

    @pl.when(c == 0)
    def _():
        cs[...] = c0_ref[...]
        ns[...] = n0_ref[...]
        ms[...] = m0_ref[...]
        qbuf[0:SUBLANES, :] = buf_ref[:, 0:MW]
        kbuf[0:SUBLANES, :] = buf_ref[:, MW:2 * MW]

    @pl.when(c > 0)
    def _():
        qbuf[0:SUBLANES, :] = qbuf[L:L + SUBLANES, :]
        kbuf[0:SUBLANES, :] = kbuf[L:L + SUBLANES, :]

    qbuf[SUBLANES:SUBLANES + L, :] = qp_ref[...]
    kbuf[SUBLANES:SUBLANES + L, :] = kp_ref[...]

    def conv(buf, off):
        base = SUBLANES - (CW - 1)
        y = cb_ref[:, off:off + MW] + cw_ref[0:1, off:off + MW] * buf[base:base + L, :]
        for j in range(1, CW):
            y = y + cw_ref[j:j + 1, off:off + MW] * buf[base + j:base + j + L, :]
        return y * _sigmoid(y)

    q = conv(qbuf, 0)
    k = conv(kbuf, MW) * (Dh ** -0.5)
    v = v_ref[...]
    heads = range(H)

    def head(x, h):
        return x[:, h * Dh:(h + 1) * Dh]

    qb = [head(q, h).astype(BF16) for h in heads]
    kb = [head(k, h).astype(BF16) for h in heads]
    vb = [head(v, h).astype(BF16) for h in heads]

    def log_sigmoid(x):
        return jnp.minimum(x, 0.0) - jnp.log1p(jnp.exp(-jnp.abs(x)))

    row = lax.broadcasted_iota(jnp.int32, (L, L), 0)
    col = lax.broadcasted_iota(jnp.int32, (L, L), 1)
    causal = col <= row
    last = lax.broadcasted_iota(jnp.int32, (L, 1), 0) == L - 1
    nt = (((1,), (1,)), ((), ()))
    tn = (((0,), (0,)), ((), ()))

    def gate_terms(h):
        ig_r = igr_ref[h:h + 1, :]
        ig_c = igc_ref[h]
        lf_r = log_sigmoid(fgr_ref[h:h + 1, :])
        lf_c = log_sigmoid(fgc_ref[h])
        bt_c = jnp.sum(jnp.where(causal, lf_r, 0.0), axis=1, keepdims=True)
        bt_r = jnp.sum(jnp.where(row <= col, lf_c, 0.0), axis=0, keepdims=True)
        m0 = ms[h]
        dmat = jnp.where(causal, bt_c - bt_r + ig_r, -jnp.inf)
        inter = bt_c + m0
        m_c = jnp.maximum(inter, jnp.max(dmat, axis=1, keepdims=True))
        m_last = jnp.sum(jnp.where(last, m_c, 0.0), axis=0, keepdims=True)
        bt_last = jnp.sum(jnp.where(last, bt_c, 0.0), axis=0, keepdims=True)
        return dict(
            m_c=m_c, m_last=m_last,
            w_intra=jnp.exp(dmat - m_c), w_inter=jnp.exp(inter - m_c),
            wk_c=jnp.exp(bt_last - bt_c + ig_c - m_last),
            decay=jnp.exp(bt_last + m0 - m_last))

    g = [gate_terms(h) for h in heads]
    s_raw = [lax.dot_general(qb[h], kb[h], nt, preferred_element_type=F32) for h in heads]
    qc = [jnp.dot(qb[h], cs[h].astype(BF16), preferred_element_type=F32) for h in heads]
    s = [s_raw[h] * g[h]["w_intra"] for h in heads]
    sv = [jnp.dot(s[h].astype(BF16), vb[h], preferred_element_type=F32) for h in heads]
    kw = [head(k, h) * g[h]["wk_c"] for h in heads]
    cu = [lax.dot_general(kw[h].astype(BF16), vb[h], tn, preferred_element_type=F32) for h in heads]

    for h in heads:
        gh = g[h]
        num = sv[h] + qc[h] * gh["w_inter"]
        den = (jnp.sum(s[h], axis=1, keepdims=True)
               + gh["w_inter"] * jnp.sum(head(q, h) * ns[h], axis=1, keepdims=True))
        den = jnp.maximum(jnp.abs(den), jnp.exp(-gh["m_c"]))
        hout = num / den
        c_new = gh["decay"] * cs[h] + cu[h]
        n_new = gh["decay"] * ns[h] + jnp.sum(kw[h], axis=0, keepdims=True)
        cs[h] = c_new
        ns[h] = n_new
        ms[h] = gh["m_last"]

        hh = _sigmoid(mo_ref[:, h * Dh:(h + 1) * Dh]) * hout
        mu = jnp.mean(hh, axis=1, keepdims=True)
        d = hh - mu
        var = jnp.mean(d * d, axis=1, keepdims=True)
        ym_ref[:, h * Dh:(h + 1) * Dh] = d * lax.rsqrt(var + LN_EPS) * gain_ref[:, h * Dh:(h + 1) * Dh]

    @pl.when(c == nc - 1)
    def _():
        c_ref[...] = cs[...]
        n_ref[...] = ns[...]
        m_ref[...] = ms[...]


def _mlstm(proj, gates, row_off, nb, seq, L, H, Dh, conv_pad, cw_pad, cb, c0, n0, m0, gain, ym_prev, CW):
    mtot = proj.shape[0]
    nc = seq // L
    rb = row_off // L
    g = gates[row_off:row_off + nb * seq]
    MW = H * Dh
    ig = g[:, 0:H].reshape(nb, nc, L, H).transpose(0, 1, 3, 2)
    fg = g[:, H:2 * H].reshape(nb, nc, L, H).transpose(0, 1, 3, 2)
    igc, fgc = ig[..., None], fg[..., None]
    aliased = ym_prev is not None

    def colspec(off):
        return pl.BlockSpec((L, MW), lambda b, c: (rb + b * nc + c, off))

    grow = pl.BlockSpec((None, None, H, L), lambda b, c: (b, c, 0, 0))
    gcol = pl.BlockSpec((None, None, H, L, 1), lambda b, c: (b, c, 0, 0, 0))
    state = lambda *tail: pl.BlockSpec((None, H) + tail, lambda b, c: (b, 0, 0, 0))
    in_specs = [
        colspec(0), colspec(1), colspec(2), colspec(3),
        grow, grow, gcol, gcol,
        pl.BlockSpec((None, SUBLANES, 2 * MW), lambda b, c: (b, 0, 0)),
        pl.BlockSpec((SUBLANES, 2 * MW), lambda b, c: (0, 0)),
        pl.BlockSpec((1, 2 * MW), lambda b, c: (0, 0)),
        state(Dh, Dh), state(1, Dh), state(1, 1),
        pl.BlockSpec((1, MW), lambda b, c: (0, 0)),
    ]
    args = [proj, proj, proj, proj, ig, fg, igc, fgc, conv_pad, cw_pad, cb, c0, n0, m0, gain]
    aliases = {}
    if aliased:
        in_specs.append(pl.BlockSpec(memory_space=pl.ANY))
        args.append(ym_prev)
        aliases = {len(args) - 1: 0}
    out_specs = [
        pl.BlockSpec((L, MW), lambda b, c: (rb + b * nc + c, 0)),
        state(Dh, Dh), state(1, Dh), state(1, 1),
    ]
    out_shape = [
        jax.ShapeDtypeStruct((mtot, MW), F32),
        jax.ShapeDtypeStruct((nb, H, Dh, Dh), F32),
        jax.ShapeDtypeStruct((nb, H, 1, Dh), F32),
        jax.ShapeDtypeStruct((nb, H, 1, 1), F32),
    ]
    return pl.pallas_call(
        functools.partial(_mlstm_kernel, L=L, H=H, Dh=Dh, CW=CW, aliased=aliased),
        grid=(nb, nc),
        in_specs=in_specs,
        out_specs=out_specs,
        out_shape=out_shape,
        scratch_shapes=[
            pltpu.VMEM((H, Dh, Dh), F32), pltpu.VMEM((H, 1, Dh), F32), pltpu.VMEM((H, 1, 1), F32),
            pltpu.VMEM((L + SUBLANES, MW), F32), pltpu.VMEM((L + SUBLANES, MW), F32),
        ],
        input_output_aliases=aliases,
        compiler_params=_params(("arbitrary", "arbitrary")),
        name="mlstm",
    )(*args)


def _sb_scores(z, mask):
    sp_pos, sp_neg = _softplus_parts(z)
    l1 = -sp_pos
    if mask is not None:
        l1 = jnp.where(mask, l1, 0.0)
    return l1, sp_neg


def _sb_newer_sum(l1, tri):
    hi = l1.astype(BF16)
    lo = (l1 - hi.astype(F32)).astype(BF16)
    return jnp.dot(hi, tri, preferred_element_type=F32) + jnp.dot(lo, tri, preferred_element_type=F32)


def _sb_weights(newer, sp_neg, carry, mask):
    a = jnp.exp((newer + carry) - sp_neg)
    if mask is not None:
        a = jnp.where(mask, a, 0.0)
    return a


def _newer_keys_matrix(n):
    row = lax.broadcasted_iota(jnp.int32, (n, n), 0)
    col = lax.broadcasted_iota(jnp.int32, (n, n), 1)
    return jnp.where(row > col, 1.0, 0.0).astype(BF16)


def _sb_prompt_kernel(bias_ref, q_ref, k_ref, v_ref, *rest, QT, KB, H, Dh, scale, aliased):
    if aliased:
        rest = rest[2:]
    o_ref, ko_ref, vo_ref, kbf, vbf, acc_ref, carry_ref = rest
    h = pl.program_id(1)
    i = pl.program_id(2)
    seq = k_ref.shape[0]

    @pl.when(i == 0)
    def _():
        k = k_ref[...]
        v = v_ref[...]
        kbf[...] = k.astype(BF16)
        vbf[...] = v.astype(BF16)
        ko_ref[pl.ds(h, seq, stride=H), :] = k
        vo_ref[pl.ds(h, seq, stride=H), :] = v

    bias = bias_ref[h]
    nsub = QT // KB
    tri = _newer_keys_matrix(KB)
    nt = (((1,), (1,)), ((), ()))

    def blocks(specs):
        zs, vs, masks = [], [], []
        for r0, j, masked in specs:
            start = pl.multiple_of(j * KB, KB)
            kj = kbf[pl.ds(start, KB), :]
            vs.append(vbf[pl.ds(start, KB), :])
            rows = QT - r0
            qb = q_ref[r0:QT, :].astype(BF16)
            zs.append(lax.dot_general(qb, kj, nt, preferred_element_type=F32) * scale + bias)
            mask = None
            if masked:
                mask = (lax.broadcasted_iota(jnp.int32, (rows, KB), 1)
                        < lax.broadcasted_iota(jnp.int32, (rows, KB), 0))
            masks.append(mask)
        scores = [_sb_scores(z, m) for z, m in zip(zs, masks)]
        newer = [_sb_newer_sum(l1, tri) for l1, _ in scores]
        for (r0, _, _), (l1, sp_neg), nw, vj, m in zip(specs, scores, newer, vs, masks):
            a = _sb_weights(nw, sp_neg, carry_ref[r0:QT, :], m)
            acc_ref[r0:QT, :] += jnp.dot(a.astype(BF16), vj, preferred_element_type=F32)
            carry_ref[r0:QT, :] += jnp.sum(l1, axis=1, keepdims=True)

    acc_ref[...] = jnp.zeros_like(acc_ref)
    carry_ref[...] = jnp.zeros_like(carry_ref)
    blocks([(d * KB, i * nsub + d, True) for d in range(nsub - 1, -1, -1)])

    def body(t, c):
        blocks([(0, i * nsub - 1 - 2 * t, False), (0, i * nsub - 2 - 2 * t, False)])
        return c

    lax.fori_loop(0, (i * nsub) // 2, body, 0)
    o_ref[...] = acc_ref[...]


def _sb_prompt(proj, bias, nb, seq, H, Dh, q_off, k_off, v_off, layer, depth, kv_prev):
    mtot = proj.shape[0]
    KB = Q_BLOCK
    QT = min(SB_Q_TILE, seq)
    assert seq % QT == 0 and (QT // KB) % 2 == 0
    nq = seq // QT
    aliased = kv_prev is not None
    kv_spec = pl.BlockSpec((None, None, seq * H, Dh), lambda b, h, i: (layer, b, 0, 0))
    kv_shape = jax.ShapeDtypeStruct((depth, nb, seq * H, Dh), F32)
    in_specs = [
        pl.BlockSpec(memory_space=pltpu.SMEM),
        pl.BlockSpec((QT, Dh), lambda b, h, i: (b * nq + i, q_off + h)),
        pl.BlockSpec((seq, Dh), lambda b, h, i: (b, k_off + h)),
        pl.BlockSpec((seq, Dh), lambda b, h, i: (b, v_off + h)),
    ]
    args = [bias, proj, proj, proj]
    aliases = {}
    if aliased:
        in_specs += [pl.BlockSpec(memory_space=pl.ANY)] * 2
        args += list(kv_prev)
        aliases = {4: 1, 5: 2}
    return pl.pallas_call(
        functools.partial(_sb_prompt_kernel, QT=QT, KB=KB, H=H, Dh=Dh, scale=Dh ** -0.5, aliased=aliased),
        grid=(nb, H, nq),
        in_specs=in_specs,
        out_specs=[pl.BlockSpec((QT, Dh), lambda b, h, i: (b * nq + i, h)), kv_spec, kv_spec],
        out_shape=[jax.ShapeDtypeStruct((mtot, H * Dh), F32), kv_shape, kv_shape],
        scratch_shapes=[
            pltpu.VMEM((seq, Dh), BF16), pltpu.VMEM((seq, Dh), BF16),
            pltpu.VMEM((QT, Dh), F32), pltpu.VMEM((QT, 1), F32),
        ],
        input_output_aliases=aliases,
        compiler_params=_params(("arbitrary", "arbitrary", "arbitrary")),
        name="sb_prompt",
    )(*args)


def _sb_sample_kernel(pt_ref, q_ref, kn_ref, vn_ref, *rest, T, H, Dh, PAGE, PP, scale):
    del pt_ref
    kc_refs, vc_refs = rest[:PP], rest[PP:2 * PP]
    bias_ref, _, o_ref, knew_ref, vnew_ref, carry_ref, acc_ref = rest[2 * PP:]
    p = pl.program_id(1)
    HT = H * T
    tri = _newer_keys_matrix(PAGE)
    nt = (((1,), (1,)), ((), ()))
    qh = [q_ref[:, h * Dh:(h + 1) * Dh].astype(BF16) for h in range(H)]

    def blocks(specs):
        zs = []
        for k_of, _, _ in specs:
            z = jnp.concatenate(
                [lax.dot_general(qh[h], k_of(h), nt, preferred_element_type=F32) for h in range(H)], axis=0)
            zs.append(z * scale + bias_ref[...])
        scores = [_sb_scores(z, m) for z, (_, _, m) in zip(zs, specs)]
        newer = [_sb_newer_sum(l1, tri) for l1, _ in scores]
        for (_, v_of, m), (l1, sp_neg), nw in zip(specs, scores, newer):
            a = _sb_weights(nw, sp_neg, carry_ref[...], m)
            out = jnp.concatenate(
                [jnp.dot(a[h * T:(h + 1) * T, :].astype(BF16), v_of(h), preferred_element_type=F32)
                 for h in range(H)], axis=0)
            acc_ref[...] += out
            carry_ref[...] += jnp.sum(l1, axis=1, keepdims=True)

    @pl.when(p == 0)
    def _():
        knew_ref[...] = jnp.zeros_like(knew_ref)
        vnew_ref[...] = jnp.zeros_like(vnew_ref)
        for h in range(H):
            knew_ref[h, 0:T, :] = kn_ref[:, h * Dh:(h + 1) * Dh]
            vnew_ref[h, 0:T, :] = vn_ref[:, h * Dh:(h + 1) * Dh]
        carry_ref[...] = jnp.zeros_like(carry_ref)
        acc_ref[...] = jnp.zeros_like(acc_ref)
        tok = lax.rem(lax.broadcasted_iota(jnp.int32, (HT, PAGE), 0), T)
        key = lax.broadcasted_iota(jnp.int32, (HT, PAGE), 1)
        blocks([(lambda h: knew_ref[h].astype(BF16), lambda h: vnew_ref[h].astype(BF16), key < tok)])

    blocks([(lambda h, i=i: kc_refs[i][pl.ds(h, PAGE, stride=H), :].astype(BF16),
             lambda h, i=i: vc_refs[i][pl.ds(h, PAGE, stride=H), :].astype(BF16), None) for i in range(PP)])

    @pl.when(p == pl.num_programs(1) - 1)
    def _():
        for h in range(H):
            o_ref[:, h * Dh:(h + 1) * Dh] = acc_ref[h * T:(h + 1) * T, :]


def _sb_sample(proj, cache_k, cache_v, page_table, bias_rows, ysb, layer, row_off, nb, T, H, Dh):
    D = H * Dh
    n_pages = page_table.shape[1]
    page = cache_k.shape[2] // H
    rb = row_off // T
    pt = page_table.reshape(-1)
    PP = max(c for c in (4, 2, 1) if n_pages % c == 0)

    def cache_spec(i):
        return pl.BlockSpec(
            (None, None, page * H, Dh),
            lambda b, p, pt_ref: (layer, pt_ref[b * n_pages + (n_pages - 1 - (p * PP + i))], 0, 0))

    grid_spec = pltpu.PrefetchScalarGridSpec(
        num_scalar_prefetch=1,
        grid=(nb, n_pages // PP),
        in_specs=[
            pl.BlockSpec((T, D), lambda b, p, pt_ref: (rb + b, 4)),
            pl.BlockSpec((T, D), lambda b, p, pt_ref: (rb + b, 5)),
            pl.BlockSpec((T, D), lambda b, p, pt_ref: (rb + b, 6)),
            *[cache_spec(i) for i in range(PP)],
            *[cache_spec(i) for i in range(PP)],
            pl.BlockSpec((H * T, page), lambda b, p, pt_ref: (0, 0)),
            pl.BlockSpec(memory_space=pl.ANY),
        ],
        out_specs=pl.BlockSpec((T, D), lambda b, p, pt_ref: (rb + b, 0)),
        scratch_shapes=[
            pltpu.VMEM((H, page, Dh), F32), pltpu.VMEM((H, page, Dh), F32),
            pltpu.VMEM((H * T, 1), F32), pltpu.VMEM((H * T, Dh), F32),
        ],
    )
    n_in = 1 + 3 + 2 * PP + 2
    return pl.pallas_call(
        functools.partial(_sb_sample_kernel, T=T, H=H, Dh=Dh, PAGE=page, PP=PP, scale=Dh ** -0.5),
        grid_spec=grid_spec,
        out_shape=jax.ShapeDtypeStruct(ysb.shape, F32),
        input_output_aliases={n_in - 1: 0},
        compiler_params=_params(("arbitrary", "arbitrary")),
        name="sb_sample",
    )(pt, proj, proj, proj, *([cache_k] * PP), *([cache_v] * PP), bias_rows, ysb)


def _layer_norm(x, g, b):
    mu = jnp.mean(x, axis=-1, keepdims=True)
    d = x - mu
    var = jnp.mean(d * d, axis=-1, keepdims=True)
    return d * lax.rsqrt(var + LN_EPS) * g + b


def _merge_kernel(ym_ref, ys_ref, gm_ref, gs_ref, w_ref, h_ref, g_ref, b_ref, o_ref, obf_ref, wbf_ref, *, alpha):
    @pl.when(pl.program_id(0) == 0)
    def _():
        wbf_ref[...] = w_ref[...].astype(BF16)

    mixed = _sigmoid(gm_ref[...]) * ym_ref[...] + _sigmoid(gs_ref[...]) * ys_ref[...]
    y = jnp.dot(mixed.astype(BF16), wbf_ref[...], preferred_element_type=F32)
    out = _layer_norm(alpha * h_ref[...] + y, g_ref[...], b_ref[...])
    o_ref[...] = out
    obf_ref[...] = out.astype(BF16)


def _merge(ym, ysb, proj, w_out, h, g, b, layer, alpha, tm, gm_blk, gs_blk):
    m, d = h.shape
    row = lambda i: (i, 0)
    vec = pl.BlockSpec((None, 1, d), lambda i: (layer, 0, 0))
    return pl.pallas_call(
        functools.partial(_merge_kernel, alpha=alpha),
        grid=(m // tm,),
        in_specs=[
            pl.BlockSpec((tm, d), row), pl.BlockSpec((tm, d), row),
            pl.BlockSpec((tm, d), lambda i: (i, gm_blk)), pl.BlockSpec((tm, d), lambda i: (i, gs_blk)),
            pl.BlockSpec((None, d, d), lambda i: (layer, 0, 0)),
            pl.BlockSpec((tm, d), row), vec, vec,
        ],
        out_specs=[pl.BlockSpec((tm, d), row), pl.BlockSpec((tm, d), row)],
        out_shape=[jax.ShapeDtypeStruct((m, d), F32), jax.ShapeDtypeStruct((m, d), BF16)],
        scratch_shapes=[pltpu.VMEM((d, d), BF16)],
        compiler_params=_params(("arbitrary",)),
        name="merge",
    )(ym, ysb, proj, proj, w_out, h, g, b)


def _swiglu_kernel(te_ref, nu_ref, x_ref, w1_ref, w3_ref, w2_ref, o_ref):
    del te_ref
    i = pl.program_id(0)
    f = pl.program_id(1)
    used = i < nu_ref[0]

    @pl.when(jnp.logical_and(f == 0, jnp.logical_not(used)))
    def _():
        o_ref[...] = jnp.zeros_like(o_ref)

    @pl.when(used)
    def _():
        x = x_ref[...]
        a = jnp.dot(x, w1_ref[...], preferred_element_type=F32)
        g = jnp.dot(x, w3_ref[...], preferred_element_type=F32)
        hmid = (a * _sigmoid(a) * g).astype(BF16)
        y = jnp.dot(hmid, w2_ref[...], preferred_element_type=F32)

        @pl.when(f == 0)
        def _():
            o_ref[...] = y

        @pl.when(f > 0)
        def _():
            o_ref[...] += y


def _swiglu(x, w1, w3, w2, tile_expert, n_used, tm, tf):
    m, d = x.shape
    ff = w1.shape[2]
    nf = ff // tf

    def fidx(i, f, nu_ref):
        return jnp.where(i < nu_ref[0], f, nf - 1)

    grid_spec = pltpu.PrefetchScalarGridSpec(
        num_scalar_prefetch=2,
        grid=(m // tm, nf),
        in_specs=[
            pl.BlockSpec((tm, d), lambda i, f, te, nu: (jnp.minimum(i, nu[0] - 1), 0)),
            pl.BlockSpec((None, d, tf), lambda i, f, te, nu: (te[i], 0, fidx(i, f, nu))),
            pl.BlockSpec((None, d, tf), lambda i, f, te, nu: (te[i], 0, fidx(i, f, nu))),
            pl.BlockSpec((None, tf, d), lambda i, f, te, nu: (te[i], fidx(i, f, nu), 0)),
        ],
        out_specs=pl.BlockSpec((tm, d), lambda i, f, te, nu: (i, 0)),
    )
    return pl.pallas_call(
        _swiglu_kernel,
        grid_spec=grid_spec,
        out_shape=jax.ShapeDtypeStruct((m, d), F32),
        compiler_params=_params(("arbitrary", "arbitrary")),
        name="swiglu",
    )(tile_expert, n_used, x, w1, w3, w2)


def _router_kernel(x_ref, w_ref, g_ref, i_ref, *, E):
    x = x_ref[...]
    w = w_ref[...]
    x_hi = x.astype(BF16)
    w_hi = w.astype(BF16)
    x_lo = (x - x_hi.astype(F32)).astype(BF16)
    w_lo = (w - w_hi.astype(F32)).astype(BF16)
    logits = (jnp.dot(x_hi, w_hi, preferred_element_type=F32) + jnp.dot(x_lo, w_hi, preferred_element_type=F32)
              + jnp.dot(x_hi, w_lo, preferred_element_type=F32))
    lane = lax.broadcasted_iota(jnp.int32, logits.shape, 1)
    lane_f = lane.astype(F32)
    valid = lane < E
    logits = jnp.where(valid, logits, -jnp.inf)
    ex = jnp.exp(logits - jnp.max(logits, axis=1, keepdims=True))
    probs = ex / jnp.sum(ex, axis=1, keepdims=True)
    p1 = jnp.max(probs, axis=1, keepdims=True)
    i1 = jnp.min(jnp.where(probs == p1, lane_f, float(LANES)), axis=1, keepdims=True)
    rest = jnp.where(jnp.logical_or(lane_f == i1, jnp.logical_not(valid)), -1.0, probs)
    p2 = jnp.max(rest, axis=1, keepdims=True)
    i2 = jnp.min(jnp.where(rest == p2, lane_f, float(LANES)), axis=1, keepdims=True)
    tot = p1 + p2
    g_ref[...] = jnp.where(lane == 0, p1 / tot, jnp.where(lane == 1, p2 / tot, 0.0))
    i_ref[...] = jnp.where(lane == 0, i1, jnp.where(lane == 1, i2, 0.0)).astype(jnp.int32)


def _router(x, w_pad, E, tm):
    m, d = x.shape
    return pl.pallas_call(
        functools.partial(_router_kernel, E=E),
        grid=(m // tm,),
        in_specs=[pl.BlockSpec((tm, d), lambda i: (i, 0)), pl.BlockSpec((d, LANES), lambda i: (0, 0))],
        out_specs=[pl.BlockSpec((tm, LANES), lambda i: (i, 0)), pl.BlockSpec((tm, LANES), lambda i: (i, 0))],
        out_shape=[jax.ShapeDtypeStruct((m, LANES), F32), jax.ShapeDtypeStruct((m, LANES), jnp.int32)],
        compiler_params=_params(("arbitrary",)),
        name="router",
    )(x, w_pad)


def _resid_ln_kernel(*refs, alpha, moe):
    if moe:
        x_ref, ya_ref, yb_ref, gt_ref, g_ref, b_ref, o_ref, obf_ref = refs
        gt = gt_ref[...]
        y = gt[:, 0:1] * ya_ref[...] + gt[:, 1:2] * yb_ref[...]
    else:
        x_ref, ya_ref, g_ref, b_ref, o_ref, obf_ref = refs
        y = ya_ref[...]
    out = _layer_norm(alpha * x_ref[...] + y, g_ref[...], b_ref[...])
    o_ref[...] = out
    obf_ref[...] = out.astype(BF16)


def _resid_ln(x, ys, gates, g, b, layer, alpha, tm):
    m, d = x.shape
    row = pl.BlockSpec((tm, d), lambda i: (i, 0))
    vec = pl.BlockSpec((None, 1, d), lambda i: (layer, 0, 0))
    moe = gates is not None
    in_specs = [row] + [row] * len(ys) + ([pl.BlockSpec((tm, LANES), lambda i: (i, 0))] if moe else []) + [vec, vec]
    args = [x, *ys] + ([gates] if moe else []) + [g, b]
    return pl.pallas_call(
        functools.partial(_resid_ln_kernel, alpha=alpha, moe=moe),
        grid=(m // tm,),
        in_specs=in_specs,
        out_specs=[row, row],
        out_shape=[jax.ShapeDtypeStruct((m, d), F32), jax.ShapeDtypeStruct((m, d), BF16)],
        compiler_params=_params(("arbitrary",)),
        name="resid_ln",
    )(*args)


def _moe_plan(idx, E, tm):
    m = idx.shape[0]
    flat_e = idx.reshape(-1)
    onehot = (flat_e[:, None] == jnp.arange(E, dtype=jnp.int32)[None, :]).astype(jnp.int32)
    ranks = jnp.cumsum(onehot, axis=0) - onehot
    rank = jnp.sum(ranks * onehot, axis=1)
    counts = jnp.sum(onehot, axis=0)
    padded = ((counts + tm - 1) // tm) * tm
    ends = jnp.cumsum(padded)
    starts = ends - padded
    dest = starts[flat_e] + rank
    n_rows = TOP_K * m + E * tm
    n_rows = ((n_rows + tm - 1) // tm) * tm
    row_token = jnp.zeros((n_rows,), jnp.int32).at[dest].set(jnp.arange(TOP_K * m, dtype=jnp.int32) // TOP_K)
    tile_start = jnp.arange(n_rows // tm, dtype=jnp.int32) * tm
    tile_expert = jnp.minimum(jnp.searchsorted(ends, tile_start, side="right"), E - 1).astype(jnp.int32)
    n_used = (ends[-1] // tm).astype(jnp.int32).reshape(1)
    return dest.reshape(m, TOP_K), row_token, tile_expert, n_used


def _ff_tile(ff, target):
    best = LANES
    for t in range(LANES, min(ff, target) + 1, LANES):
        if ff % t == 0:
            best = t
    return best


def kernel(x_prompt, x_sample, cache_k, cache_v, page_table, state_C, state_n, state_m, state_conv, w_in, b_in, conv_w, conv_b, mlstm_b_f, mlstm_gain, sb_bias, w_out, ln1_g, ln1_b, ln2_g, ln2_b, ffn_w1, ffn_w3, ffn_w2, router_w, exp_w1, exp_w3, exp_w2):
    B, S, D = x_prompt.shape
    DB, T, _ = x_sample.shape
    depth = w_in.shape[0]
    H = mlstm_b_f.shape[1]
    Dh = D // H
    SH = sb_bias.shape[1]
    SDh = D // SH
    CW = conv_w.shape[1]
    E = router_w.shape[2]
    n_pool, page = cache_k.shape[1], cache_k.shape[2]
    alpha = (2 * depth) ** 0.25
    n_main = 9 * D
    assert w_in.shape[2] == n_main + 2 * H and 2 * H <= LANES
    mp, ms = B * S, DB * T
    mtot = mp + ms
    tm_lin = _row_tile(mtot, 1280)
    tm_row = _row_tile(mtot, 640)
    tm_moe = 512
    tn_lin = _ff_tile(n_main, 1024)

    h = jnp.concatenate([x_prompt.reshape(mp, D), x_sample.reshape(ms, D)], axis=0)
    hbf = h.astype(BF16)
    cache_k = cache_k.reshape(depth, n_pool, page * SH, SDh)
    cache_v = cache_v.reshape(depth, n_pool, page * SH, SDh)
    b_in3 = b_in.reshape(depth, 1, -1)
    ln1_g3, ln1_b3 = ln1_g.reshape(depth, 1, D), ln1_b.reshape(depth, 1, D)
    ln2_g3, ln2_b3 = ln2_g.reshape(depth, 1, D), ln2_b.reshape(depth, 1, D)
    ffn_w1b, ffn_w3b, ffn_w2b = ffn_w1.astype(BF16), ffn_w3.astype(BF16), ffn_w2.astype(BF16)
    exp_w1b, exp_w3b, exp_w2b = exp_w1.astype(BF16), exp_w3.astype(BF16), exp_w2.astype(BF16)
    zero_c = jnp.zeros((B, H, Dh, Dh), F32)
    zero_n = jnp.zeros((B, H, 1, Dh), F32)
    zero_m = jnp.zeros((B, H, 1, 1), F32)
    zero_conv = jnp.zeros((B, SUBLANES, 2 * D), F32)
    dense_te = jnp.zeros((mtot // tm_row,), jnp.int32)
    dense_nu = jnp.full((1,), mtot // tm_row, jnp.int32)

    outs = {k: [] for k in ("ks", "vs", "cp", "np", "mp", "bp", "cs", "ns", "ms", "bs")}
    kv_new = None
    for l in range(depth):
        proj = _linear(hbf, w_in, b_in3, l, n_main, tm_lin, tn_lin)
        wg = jnp.pad(w_in[l][:, n_main:], ((0, 0), (0, LANES - 2 * H)))[None]
        bg = jnp.pad(b_in[l][n_main:] + jnp.concatenate([jnp.zeros((H,), F32), mlstm_b_f[l]]), (0, LANES - 2 * H))
        gates = _linear(hbf, wg, bg.reshape(1, 1, LANES), 0, LANES, tm_lin, LANES)

        cw_pad = jnp.pad(conv_w[l], ((0, SUBLANES - CW), (0, 0)))
        cb = conv_b[l].reshape(1, -1)
        gain = mlstm_gain[l].reshape(1, -1)
        conv_s = jnp.pad(state_conv[l], ((0, 0), (SUBLANES - (CW - 1), 0), (0, 0)))
        ym, c_p, n_p, m_p = _mlstm(proj, gates, 0, B, S, CHUNK, H, Dh, zero_conv, cw_pad, cb,
                                   zero_c, zero_n, zero_m, gain, None, CW)
        ym, c_s, n_s, m_s = _mlstm(proj, gates, mp, DB, T, T, H, Dh, conv_s, cw_pad, cb,
                                   state_C[l], state_n[l].reshape(DB, H, 1, Dh), state_m[l].reshape(DB, H, 1, 1),
                                   gain, ym, CW)

        ysb, k_new, v_new = _sb_prompt(proj, sb_bias[l], B, S, SH, SDh, 4 * SH, 5 * SH, 6 * SH, l, depth, kv_new)
        kv_new = (k_new, v_new)
        bias_rows = jnp.broadcast_to(jnp.repeat(sb_bias[l], T)[:, None], (SH * T, page))
        ysb = _sb_sample(proj, cache_k, cache_v, page_table, bias_rows, ysb, l, mp, DB, T, SH, SDh)

        h, hbf = _merge(ym, ysb, proj, w_out, h, ln1_g3, ln1_b3, l, alpha, tm_row, 7, 8)

        j = l // 2
        if l % 2 == 0:
            tf = _ff_tile(ffn_w1.shape[2], 1408)
            y = _swiglu(hbf, ffn_w1b[j:j + 1], ffn_w3b[j:j + 1], ffn_w2b[j:j + 1], dense_te, dense_nu, tm_row, tf)
            h, hbf = _resid_ln(h, [y], None, ln2_g3, ln2_b3, l, alpha, tm_row)
        else:
            wr = jnp.pad(router_w[j], ((0, 0), (0, LANES - E)))
            gts, idx = _router(h, wr, E, tm_row)
            pos, row_token, tile_expert, n_used = _moe_plan(idx[:, :TOP_K], E, tm_moe)
            xs = jnp.take(hbf, row_token, axis=0)
            tf = _ff_tile(exp_w1.shape[3], 512)
            ysorted = _swiglu(xs, exp_w1b[j], exp_w3b[j], exp_w2b[j], tile_expert, n_used, tm_moe, tf)
            ya = jnp.take(ysorted, pos[:, 0], axis=0)
            yb = jnp.take(ysorted, pos[:, 1], axis=0)
            h, hbf = _resid_ln(h, [ya, yb], gts, ln2_g3, ln2_b3, l, alpha, tm_row)

        kcol, vcol = slice(5 * D, 6 * D), slice(6 * D, 7 * D)
        outs["ks"].append(proj[mp:, kcol].reshape(DB, T, SH, SDh))
        outs["vs"].append(proj[mp:, vcol].reshape(DB, T, SH, SDh))
        outs["cp"].append(c_p)
        outs["np"].append(n_p.reshape(B, H, Dh))
        outs["mp"].append(m_p.reshape(B, H))
        qk_p = proj[:mp, :2 * D].reshape(B, S, 2 * D)
        outs["bp"].append(qk_p[:, S - (CW - 1):])
        outs["cs"].append(c_s)
        outs["ns"].append(n_s.reshape(DB, H, Dh))
        outs["ms"].append(m_s.reshape(DB, H))
        qk_s = proj[mp:, :2 * D].reshape(DB, T, 2 * D)
        outs["bs"].append(jnp.concatenate([state_conv[l], qk_s], axis=1)[:, -(CW - 1):])

    st = {k: jnp.stack(v) for k, v in outs.items()}
    return (h[:mp].reshape(B, S, D), h[mp:].reshape(DB, T, D),
            kv_new[0].reshape(depth, B, S, SH, SDh), kv_new[1].reshape(depth, B, S, SH, SDh), st["ks"], st["vs"],
            st["cp"], st["np"], st["mp"], st["bp"],
            st["cs"], st["ns"], st["ms"], st["bs"])
```
